```python
import math
import jax
import jax.numpy as jnp
from jax import lax
import numpy as np

D_MODEL = 1024
BATCH = 8
SEQ = 2048
DEPTH = 2
DEC_BATCH = 128
DEC_SEQ = 4
PAST_LEN = 16384
PAGE_SIZE = 128

EPS = 1e-6
CHUNK = 64
CONV_W = 4
N_BRANCH = 3
BRANCH_W = D_MODEL // 2

GLA_DK = 128
GLA_DV = 128
GLA_HEADS = BRANCH_W // GLA_DV
GLA_RANK = 16
GLA_GATE_TEMP = 16.0

SSD_HEADDIM = 64
SSD_HEADS = BRANCH_W // SSD_HEADDIM
SSD_GROUPS = 2
SSD_REP = SSD_HEADS // SSD_GROUPS
SSD_STATE = 128
SSD_INNER = SSD_HEADS * SSD_HEADDIM
SSD_CONV_CH = SSD_INNER + 2 * SSD_GROUPS * SSD_STATE

GDN_DK = 128
GDN_DV = 128
GDN_HEADS = BRANCH_W // GDN_DV
GDN_CONV_CH = GDN_HEADS * (2 * GDN_DK + GDN_DV)

FFN_HIDDEN = -(-(8 * D_MODEL) // (3 * 256)) * 256

IN_WIDTHS = (GLA_HEADS * GLA_DK, GLA_HEADS * GLA_DK, GLA_HEADS * GLA_DV, GLA_HEADS * GLA_DV, GLA_RANK,
             SSD_INNER, SSD_CONV_CH, SSD_HEADS,
             GDN_CONV_CH, GDN_HEADS, GDN_HEADS, GDN_HEADS * GDN_DV,
             N_BRANCH * D_MODEL)
N_IN = sum(IN_WIDTHS)
SPLIT_IDX = tuple(sum(IN_WIDTHS[:i + 1]) for i in range(len(IN_WIDTHS) - 1))

kernel_name = 'hybrid_gla_ssd_gdn_decoder_step'


def _rms_norm(x, g):
    xf = x.astype(jnp.float32)
    y = xf * lax.rsqrt(jnp.mean(xf * xf, axis=-1, keepdims=True) + EPS)
    return (y * g.astype(jnp.float32)).astype(x.dtype)


def _l2_norm(x):
    return x * lax.rsqrt(jnp.sum(x * x, axis=-1, keepdims=True) + EPS)


def _causal_conv(x, buf, w, b=None):
    L = x.shape[1]
    xp = jnp.concatenate([buf.astype(x.dtype), x], axis=1)
    y = xp[:, 0:L] * w[0]
    for j in range(1, CONV_W):
        y = y + xp[:, j:j + L] * w[j]
    if b is not None:
        y = y + b
    return y, xp[:, L:]


def _to_chunks(t, c):
    return jnp.moveaxis(t.reshape(t.shape[0], t.shape[1] // c, c, *t.shape[2:]), 1, 0)


def _from_chunks(t):
    t = jnp.moveaxis(t, 0, 1)
    return t.reshape(t.shape[0], t.shape[1] * t.shape[2], *t.shape[3:])


def _gla_chunked(q, k, v, log_a, S0):
    C = math.gcd(q.shape[1], CHUNK)
    mask = jnp.tril(jnp.ones((C, C), bool))

    def step(S, inp):
        qi, ki, vi, gi = inp
        b = jnp.cumsum(gi, axis=1)
        qd = qi * jnp.exp(b)
        kd = ki * jnp.exp(-b)
        A = jnp.where(mask, jnp.einsum('bthk,bshk->bhts', qd, kd), 0.0)
        o = jnp.einsum('bhts,bshv->bthv', A, vi) + jnp.einsum('bthk,bhkv->bthv', qd, S)
        bl = b[:, -1]
        S = S * jnp.exp(bl)[..., None] + jnp.einsum('bshk,bshv->bhkv', ki * jnp.exp(bl[:, None] - b), vi)
        return S, o

    S, o = lax.scan(step, S0.astype(jnp.float32), tuple(_to_chunks(t, C) for t in (q, k, v, log_a)))
    return _from_chunks(o), S


def _ssd_chunked(x, dt, A, Bm, Cm, S0):
    C = math.gcd(x.shape[1], CHUNK)
    mask = jnp.tril(jnp.ones((C, C), bool))[None, :, :, None, None]

    def step(S, inp):
        xi, dti, Bi, Ci = inp
        cum = jnp.cumsum(dti * A, axis=1)
        seg = cum[:, :, None] - cum[:, None, :]
        Lm = jnp.exp(jnp.where(mask, seg, -jnp.inf))
        CB = jnp.einsum('btgn,bsgn->btsg', Ci, Bi)
        M = CB[..., None] * Lm * dti[:, None]
        y = jnp.einsum('btsgr,bsgrp->btgrp', M, xi)
        y = y + jnp.einsum('btgn,bgrpn->btgrp', Ci, S) * jnp.exp(cum)[..., None]
        last = cum[:, -1]
        wts = jnp.exp(last[:, None] - cum) * dti
        S = S * jnp.exp(last)[..., None, None] + jnp.einsum('bsgn,bsgr,bsgrp->bgrpn', Bi, wts, xi)
        return S, y

    S, y = lax.scan(step, S0.astype(jnp.float32), tuple(_to_chunks(t, C) for t in (x, dt, Bm, Cm)))
    return _from_chunks(y), S


def _gdn_chunked(q, k, v, g, beta, S0):
    C = math.gcd(q.shape[1], CHUNK)
    V = v.shape[-1]
    mask = jnp.tril(jnp.ones((C, C), bool))
    strict = jnp.tril(jnp.ones((C, C), bool), k=-1)
    eye = jnp.eye(C, dtype=jnp.float32)

    def step(S, inp):
        qi, ki, vi, gi, bi = inp
        cum = jnp.cumsum(gi, axis=1)
        cum_h = jnp.swapaxes(cum, 1, 2)
        seg = cum_h[..., :, None] - cum_h[..., None, :]
        Lm = jnp.exp(jnp.where(mask, seg, -jnp.inf))
        Akk = jnp.where(strict, jnp.einsum('bthk,bshk,bsh->bhts', ki, ki, bi) * Lm, 0.0)
        Akk = jnp.einsum('bhts,bth->bhts', Akk, bi)
        IA = Akk + eye
        vb = jnp.einsum('bthv,bth->bhtv', vi, bi)
        kbd = jnp.einsum('bthk,bth->bhtk', ki, bi * jnp.exp(cum))
        sol = lax.linalg.triangular_solve(IA, jnp.concatenate([vb, kbd], axis=-1),
                                          left_side=True, lower=True, unit_diagonal=True)
        U, W = sol[..., :V], sol[..., V:]
        vnew = U - jnp.einsum('bhtk,bhkv->bhtv', W, S)
        Aqk = jnp.where(mask, jnp.einsum('bthk,bshk->bhts', qi, ki) * Lm, 0.0)
        o = (jnp.einsum('bthk,bhkv->bthv', qi * jnp.exp(cum)[..., None], S)
             + jnp.einsum('bhts,bhsv->bthv', Aqk, vnew))
        last = cum_h[..., -1]
        S = (S * jnp.exp(last)[..., None, None]
             + jnp.einsum('bshk,bhs,bhsv->bhkv', ki, jnp.exp(last[..., None] - cum_h), vnew))
        return S, o

    S, o = lax.scan(step, S0.astype(jnp.float32), tuple(_to_chunks(t, C) for t in (q, k, v, g, beta)))
    return _from_chunks(o), S


def _token_mixers(h, p, l, st):
    s_gla, s_ssd, cv_ssd, s_gdn, cv_gdn = st
    Bn, L, _ = h.shape
    u = jnp.einsum('bld,dn->bln', h, p['w_in'][l]).astype(jnp.float32)
    (gq, gk, gv, gr, glr, sz, sxbc, sdt, dqkv, da, db, dg, mg) = jnp.split(u, SPLIT_IDX, axis=-1)

    q = gq.reshape(Bn, L, GLA_HEADS, GLA_DK) * (GLA_DK ** -0.5)
    k = gk.reshape(Bn, L, GLA_HEADS, GLA_DK)
    v = gv.reshape(Bn, L, GLA_HEADS, GLA_DV)
    logit = jnp.einsum('blr,rn->bln', glr, p['w_gla_gate'][l]) + p['b_gla_gate'][l]
    log_a = (jax.nn.log_sigmoid(logit.astype(jnp.float32)) / GLA_GATE_TEMP).reshape(Bn, L, GLA_HEADS, GLA_DK)
    o, s_gla_new = _gla_chunked(q, k, v, log_a, s_gla)
    y_gla = (_rms_norm(o, p['g_gla_norm'][l]) * jax.nn.silu(gr.reshape(Bn, L, GLA_HEADS, GLA_DV))).reshape(Bn, L, BRANCH_W)

    xbc, cv_ssd_new = _causal_conv(sxbc, cv_ssd, p['w_ssd_conv'][l], p['b_ssd_conv'][l])
    xbc = jax.nn.silu(xbc)
    sx, sB, sC = jnp.split(xbc, (SSD_INNER, SSD_INNER + SSD_GROUPS * SSD_STATE), axis=-1)
    xs = sx.reshape(Bn, L, SSD_GROUPS, SSD_REP, SSD_HEADDIM)
    Bm = sB.reshape(Bn, L, SSD_GROUPS, SSD_STATE)
    Cm = sC.reshape(Bn, L, SSD_GROUPS, SSD_STATE)
    dt = jax.nn.softplus(sdt + p['ssd_dt_bias'][l]).reshape(Bn, L, SSD_GROUPS, SSD_REP)
    A = -jnp.exp(p['ssd_a_log'][l].astype(jnp.float32)).reshape(SSD_GROUPS, SSD_REP)
    S0 = s_ssd.reshape(Bn, SSD_GROUPS, SSD_REP, SSD_HEADDIM, SSD_STATE)
    ys, S_new = _ssd_chunked(xs, dt, A, Bm, Cm, S0)
    ys = ys + xs * p['ssd_d'][l].reshape(SSD_GROUPS, SSD_REP)[..., None]
    ys = ys * jax.nn.silu(sz.reshape(Bn, L, SSD_GROUPS, SSD_REP, SSD_HEADDIM))
    ys = _rms_norm(ys.reshape(Bn, L, SSD_GROUPS, SSD_REP * SSD_HEADDIM),
                   p['g_ssd_norm'][l].reshape(SSD_GROUPS, SSD_REP * SSD_HEADDIM))
    y_ssd = ys.reshape(Bn, L, BRANCH_W)
    s_ssd_new = S_new.reshape(Bn, SSD_HEADS, SSD_HEADDIM, SSD_STATE)

    qkv, cv_gdn_new = _causal_conv(dqkv, cv_gdn, p['w_gdn_conv'][l])
    qkv = jax.nn.silu(qkv)
    dq, dk, dv = jnp.split(qkv, (GDN_HEADS * GDN_DK, 2 * GDN_HEADS * GDN_DK), axis=-1)
    qd = _l2_norm(dq.reshape(Bn, L, GDN_HEADS, GDN_DK)) * (GDN_DK ** -0.5)
    kd = _l2_norm(dk.reshape(Bn, L, GDN_HEADS, GDN_DK))
    vd = dv.reshape(Bn, L, GDN_HEADS, GDN_DV)
    g = -jnp.exp(p['gdn_a_log'][l].astype(jnp.float32)) * jax.nn.softplus(da + p['gdn_dt_bias'][l])
    beta = jax.nn.sigmoid(db)
    od, s_gdn_new = _gdn_chunked(qd, kd, vd, g, beta, s_gdn)
    y_gdn = (_rms_norm(od, p['g_gdn_norm'][l]) * jax.nn.silu(dg.reshape(Bn, L, GDN_HEADS, GDN_DV))).reshape(Bn, L, BRANCH_W)

    ys_all = jnp.stack([y_gla, y_ssd, y_gdn]).astype(h.dtype)
    yb = jnp.einsum('nblw,nwd->nbld', ys_all, p['w_branch'][l])
    gates = jax.nn.sigmoid(mg).reshape(Bn, L, N_BRANCH, D_MODEL).astype(h.dtype)
    merged = jnp.einsum('blnd,nbld->bld', gates, yb)
    out = jnp.einsum('bld,de->ble', merged, p['w_out'][l])
    return out, (s_gla_new, s_ssd_new, cv_ssd_new, s_gdn_new, cv_gdn_new)


def _layer(x, c, p, l, st):
    mod = jnp.einsum('bd,de->be', jax.nn.silu(c), p['w_ada'][l]) + p['b_ada'][l]
    sh_m, sc_m, gt_m, sh_f, sc_f, gt_f = jnp.split(mod[:, None, :], 6, axis=-1)
    h = _rms_norm(x, p['g_pre_mix'][l]) * (1.0 + sc_m) + sh_m
    mix, new_st = _token_mixers(h, p, l, st)
    x = x + gt_m * _rms_norm(mix, p['g_post_mix'][l])
    h = _rms_norm(x, p['g_pre_ffn'][l]) * (1.0 + sc_f) + sh_f
    a, b = jnp.split(jnp.einsum('bld,df->blf', h, p['w_ffn_in'][l]), 2, axis=-1)
    f = jnp.einsum('blf,fd->bld', jax.nn.silu(a) * b, p['w_ffn_out'][l])
    x = x + gt_f * _rms_norm(f, p['g_post_ffn'][l])
    return x, new_st


def _trunk(x, c, p, states):
    outs = ([], [], [], [], [])
    for l in range(DEPTH):
        x, ns = _layer(x, c, p, l, tuple(s[l] for s in states))
        for lst, s in zip(outs, ns):
            lst.append(s)
    return x, tuple(jnp.stack(lst) for lst in outs)


def _zero_states(n):
    f = jnp.float32
    return (jnp.zeros((DEPTH, n, GLA_HEADS, GLA_DK, GLA_DV), f),
            jnp.zeros((DEPTH, n, SSD_HEADS, SSD_HEADDIM, SSD_STATE), f),
            jnp.zeros((DEPTH, n, CONV_W - 1, SSD_CONV_CH), f),
            jnp.zeros((DEPTH, n, GDN_HEADS, GDN_DK, GDN_DV), f),
            jnp.zeros((DEPTH, n, CONV_W - 1, GDN_CONV_CH), f))


def setup_inputs(seed: int = 0) -> dict:
    key = jax.random.key(seed)
    kit = iter(jax.random.split(key, 64))
    f32 = jnp.float32
    D = D_MODEL

    def nrm(shape, scale):
        return jax.random.normal(next(kit), shape, f32) * scale

    def gain(shape):
        return 1.0 + nrm(shape, 0.05)

    def a_log(shape):
        return jnp.log(jax.random.uniform(next(kit), shape, f32, 1.0, 16.0))

    def dt_bias(shape):
        u = jax.random.uniform(next(kit), shape, f32)
        dt = jnp.exp(u * (math.log(0.1) - math.log(0.001)) + math.log(0.001))
        return dt + jnp.log(-jnp.expm1(-dt))

    return {
        'x_prompt': nrm((BATCH, SEQ, D), 1.0),
        'x_sample': nrm((DEC_BATCH, DEC_SEQ, D), 1.0),
        'state_gla': nrm((DEPTH, DEC_BATCH, GLA_HEADS, GLA_DK, GLA_DV), 0.5),
        'state_ssd': nrm((DEPTH, DEC_BATCH, SSD_HEADS, SSD_HEADDIM, SSD_STATE), 0.5),
        'cache_ssd_conv': nrm((DEPTH, DEC_BATCH, CONV_W - 1, SSD_CONV_CH), 1.0),
        'state_gdn': nrm((DEPTH, DEC_BATCH, GDN_HEADS, GDN_DK, GDN_DV), 0.5),
        'cache_gdn_conv': nrm((DEPTH, DEC_BATCH, CONV_W - 1, GDN_CONV_CH), 1.0),
        'c_prompt': nrm((BATCH, D), 1.0),
        'c_sample': nrm((DEC_BATCH, D), 1.0),
        'w_ada': nrm((DEPTH, D, 6 * D), 0.5 * D ** -0.5),
        'b_ada': nrm((DEPTH, 6 * D), 0.02),
        'g_pre_mix': gain((DEPTH, D)),
        'g_post_mix': gain((DEPTH, D)),
        'g_pre_ffn': gain((DEPTH, D)),
        'g_post_ffn': gain((DEPTH, D)),
        'w_in': nrm((DEPTH, D, N_IN), D ** -0.5),
        'w_gla_gate': nrm((DEPTH, GLA_RANK, GLA_HEADS * GLA_DK), GLA_RANK ** -0.5),
        'b_gla_gate': nrm((DEPTH, GLA_HEADS * GLA_DK), 0.1),
        'g_gla_norm': gain((DEPTH, GLA_DV)),
        'w_ssd_conv': nrm((DEPTH, CONV_W, SSD_CONV_CH), CONV_W ** -0.5),
        'b_ssd_conv': nrm((DEPTH, SSD_CONV_CH), 0.05),
        'ssd_dt_bias': dt_bias((DEPTH, SSD_HEADS)),
        'ssd_a_log': a_log((DEPTH, SSD_HEADS)),
        'ssd_d': gain((DEPTH, SSD_HEADS)),
        'g_ssd_norm': gain((DEPTH, SSD_INNER)),
        'w_gdn_conv': nrm((DEPTH, CONV_W, GDN_CONV_CH), CONV_W ** -0.5),
        'gdn_dt_bias': dt_bias((DEPTH, GDN_HEADS)),
        'gdn_a_log': a_log((DEPTH, GDN_HEADS)),
        'g_gdn_norm': gain((DEPTH, GDN_DV)),
        'w_branch': nrm((DEPTH, N_BRANCH, BRANCH_W, D), BRANCH_W ** -0.5),
        'w_out': nrm((DEPTH, D, D), D ** -0.5),
        'w_ffn_in': nrm((DEPTH, D, 2 * FFN_HIDDEN), D ** -0.5),
        'w_ffn_out': nrm((DEPTH, FFN_HIDDEN, D), FFN_HIDDEN ** -0.5),
    }


def reference(x_prompt, x_sample, state_gla, state_ssd, cache_ssd_conv, state_gdn, cache_gdn_conv,
              c_prompt, c_sample, w_ada, b_ada, g_pre_mix, g_post_mix, g_pre_ffn, g_post_ffn,
              w_in, w_gla_gate, b_gla_gate, g_gla_norm, w_ssd_conv, b_ssd_conv, ssd_dt_bias,
              ssd_a_log, ssd_d, g_ssd_norm, w_gdn_conv, gdn_dt_bias, gdn_a_log, g_gdn_norm,
              w_branch, w_out, w_ffn_in, w_ffn_out):
    p = dict(w_ada=w_ada, b_ada=b_ada, g_pre_mix=g_pre_mix, g_post_mix=g_post_mix,
             g_pre_ffn=g_pre_ffn, g_post_ffn=g_post_ffn, w_in=w_in, w_gla_gate=w_gla_gate,
             b_gla_gate=b_gla_gate, g_gla_norm=g_gla_norm, w_ssd_conv=w_ssd_conv,
             b_ssd_conv=b_ssd_conv, ssd_dt_bias=ssd_dt_bias, ssd_a_log=ssd_a_log, ssd_d=ssd_d,
             g_ssd_norm=g_ssd_norm, w_gdn_conv=w_gdn_conv, gdn_dt_bias=gdn_dt_bias,
             gdn_a_log=gdn_a_log, g_gdn_norm=g_gdn_norm, w_branch=w_branch, w_out=w_out,
             w_ffn_in=w_ffn_in, w_ffn_out=w_ffn_out)
    y_prompt, (pg, ps, pcs, pd, pcd) = _trunk(x_prompt, c_prompt, p, _zero_states(x_prompt.shape[0]))
    y_sample, (sg, ss, scs, sd, scd) = _trunk(
        x_sample, c_sample, p, (state_gla, state_ssd, cache_ssd_conv, state_gdn, cache_gdn_conv))
    return (y_prompt, y_sample, pg, ps, pcs, pd, pcd, sg, ss, scs, sd, scd)
```

```python
import functools

import jax
import jax.numpy as jnp
from jax import lax
from jax.experimental import pallas as pl
from jax.experimental.pallas import tpu as pltpu

F32 = jnp.float32
BF16 = jnp.bfloat16

EPS = 1e-6
CONV_W = 4
N_BRANCH = 3
GLA_RANK = 16
GLA_GATE_TEMP = 16.0
HEAD_W = 128
N_HEADS = 4
SSD_HEADS = 8
SSD_HEADDIM = 64
SSD_GROUPS = 2
BRANCH_W = N_HEADS * HEAD_W

TILE = 64
SAMPLE_SEG = 8
LANES = 128
SM_DT = GLA_RANK
SM_DA = SM_DT + SSD_HEADS
SM_DB = SM_DA + N_HEADS

VMEM_LIMIT = 56 * 1024 * 1024


def _sigmoid(x):
    return 1.0 / (1.0 + jnp.exp(-x))


def _silu(x):
    return x * _sigmoid(x)


def _softplus(x):
    return jnp.maximum(x, 0.0) + jnp.log1p(jnp.exp(-jnp.abs(x)))


def _dot(a, b, dims=(((1,), (0,)), ((), ()))):
    return lax.dot_general(a, b, dims, preferred_element_type=F32)


def _mm(a, b):
    return _dot(a.astype(BF16), b.astype(BF16))


def _mm_nt(a, b):
    return _dot(a.astype(BF16), b.astype(BF16), (((1,), (1,)), ((), ())))


def _split(a):
    hi = a.astype(BF16)
    lo = (a - hi.astype(F32)).astype(BF16)
    return hi, lo


def _mm01(m01, a):
    hi, lo = _split(a)
    return _dot(m01, hi) + _dot(m01, lo)


def _mm3(a, b):
    ahi, alo = _split(a)
    bhi, blo = _split(b)
    return _dot(ahi, bhi) + (_dot(ahi, blo) + _dot(alo, bhi))


def _rms(v, g):
    return v * lax.rsqrt(jnp.mean(v * v, axis=-1, keepdims=True) + EPS) * g


class _TileGeom:
    def __init__(self, seg_len, real_lo):
        self.seg_len = seg_len
        self.nseg = TILE // seg_len
        self.n_real = seg_len - real_lo
        r = lax.broadcasted_iota(jnp.int32, (TILE, TILE), 0)
        c = lax.broadcasted_iota(jnp.int32, (TILE, TILE), 1)
        rcol = lax.broadcasted_iota(jnp.int32, (TILE, 1), 0)
        crow = lax.broadcasted_iota(jnp.int32, (1, TILE), 1)
        if self.nseg == 1:
            self.causal = r >= c
            self.strict = r > c
            self.all01 = jnp.ones((TILE, TILE), BF16)
            self.valid = None
            self.row_in_seg = [None]
            self.col_in_seg = [None]
        else:
            same = (r // seg_len) == (c // seg_len)
            self.causal = same & (r >= c)
            self.strict = same & (r > c)
            self.all01 = jnp.where(same, 1.0, 0.0).astype(BF16)
            self.valid = (rcol % seg_len) >= real_lo
            self.row_in_seg = [(rcol // seg_len) == s for s in range(self.nseg)]
            self.col_in_seg = [(crow // seg_len) == s for s in range(self.nseg)]
        self.causal01 = jnp.where(self.causal, 1.0, 0.0).astype(BF16)
        self.eye = jnp.where(r == c, 1.0, 0.0).astype(F32)

    def keep_valid(self, v):
        return v if self.valid is None else jnp.where(self.valid, v, 0.0)

    def rows(self, s, v):
        m = self.row_in_seg[s]
        return v if m is None else jnp.where(m, v, 0.0)

    def cols(self, s, v):
        m = self.col_in_seg[s]
        return v if m is None else jnp.where(m, v, 0.0)

    def seg_scalar(self, s, v, lane):
        r0 = s * self.seg_len
        return v[r0:r0 + 1, lane:lane + 1]


def _conv_tile(xin, hist, w, bias):
    acc = xin * w[CONV_W - 1:CONV_W, :]
    row8 = lax.broadcasted_iota(jnp.int32, (SAMPLE_SEG, 1), 0)
    for j in range(1, CONV_W):
        r = pltpu.roll(xin, j, axis=0)
        hr = pltpu.roll(hist, j, axis=0)
        first = jnp.where(row8 < j, hr, r[:SAMPLE_SEG])
        shifted = jnp.concatenate([first, r[SAMPLE_SEG:]], axis=0)
        acc = acc + shifted * w[CONV_W - 1 - j:CONV_W - j, :]
    if bias is not None:
        acc = acc + bias
    return acc


def _conv_input(geom, u_rows, cache_rows):
    if geom.valid is None:
        return u_rows
    return jnp.where(geom.valid, u_rows, cache_rows)


def _ada_kernel(c_ref, w_ref, b_ref, o_ref):
    a = _silu(c_ref[...])
    o_ref[0] = _mm(a, w_ref[0]) + b_ref[0]


def _ada_call(c_all, w_ada, b_ada):
    depth, d, n = w_ada.shape
    rows = c_all.shape[0]
    tn = 1536
    return pl.pallas_call(
        _ada_kernel,
        out_shape=jax.ShapeDtypeStruct((depth, rows, n), F32),
        grid=(depth, n // tn),
        in_specs=[
            pl.BlockSpec((rows, d), lambda l, j: (0, 0)),
            pl.BlockSpec((1, d, tn), lambda l, j: (l, 0, j)),
            pl.BlockSpec((1, 1, tn), lambda l, j: (l, 0, j)),
        ],
        out_specs=pl.BlockSpec((1, rows, tn), lambda l, j: (l, 0, j)),
        compiler_params=pltpu.CompilerParams(
            dimension_semantics=("arbitrary", "arbitrary"), vmem_limit_bytes=VMEM_LIMIT),
        name="ada_mod",
    )(c_all, w_ada, b_ada.reshape(depth, 1, n))


def _inproj_kernel(x_ref, sh_ref, sc_ref, g_ref, w_ref, u_ref):
    x = x_ref[...]
    nb, r, d = x.shape
    h = _rms(x, g_ref[...]) * (1.0 + sc_ref[...]) + sh_ref[...]
    u_ref[...] = _dot(h.reshape(nb * r, d).astype(BF16), w_ref[...])


def _row_blocking(x3, max_rows):
    nb, r, _ = x3.shape
    if r >= max_rows:
        assert r % max_rows == 0
        return 1, max_rows
    nbk = min(nb, max_rows // r)
    assert nb % nbk == 0
    return nbk, r


def _inproj_call(x3, mod3, g_pre, w_in_r):
    nb, r, d = x3.shape
    n = w_in_r.shape[1]
    tn = n // 3
    nbk, rk = _row_blocking(x3, 512)
    gi, gj = nb // nbk, r // rk
    d_blk = lambda col: pl.BlockSpec((nbk, 1, d), lambda c, i, j, col=col: (i, 0, col))
    return pl.pallas_call(
        _inproj_kernel,
        out_shape=jax.ShapeDtypeStruct((nb * r, n), F32),
        grid=(3, gi, gj),
        in_specs=[
            pl.BlockSpec((nbk, rk, d), lambda c, i, j: (i, j, 0)),
            d_blk(0), d_blk(1),
            pl.BlockSpec((1, 1, d), lambda c, i, j: (0, 0, 0)),
            pl.BlockSpec((d, tn), lambda c, i, j: (0, c)),
        ],
        out_specs=pl.BlockSpec((nbk * rk, tn), lambda c, i, j: (i * gj + j, c)),
        compiler_params=pltpu.CompilerParams(
            dimension_semantics=("arbitrary", "arbitrary", "arbitrary"), vmem_limit_bytes=VMEM_LIMIT),
        name="in_proj",
    )(x3, mod3, mod3, g_pre.reshape(1, 1, d), w_in_r)


def _state_access(prompt, geom, s_in_ref, s_out_ref, scr_ref, transpose):
    if prompt:
        def get(s, h):
            return scr_ref[h]

        def put(s, h, val):
            scr_ref[h] = val
    else:
        def get(s, h):
            v = s_in_ref[s, h]
            return v.T if transpose else v

        def put(s, h, val):
            s_out_ref[s, h] = val.T if transpose else val
    return get, put


def _prompt_state_prologue(scr_ref, hist_refs):
    @pl.when(pl.program_id(1) == 0)
    def _():
        scr_ref[...] = jnp.zeros(scr_ref.shape, F32)
        for hr in hist_refs:
            hr[...] = jnp.zeros(hr.shape, F32)


def _prompt_state_epilogue(scr_ref, s_out_ref, transpose):
    @pl.when(pl.program_id(1) == pl.num_programs(1) - 1)
    def _():
        for h in range(N_HEADS):
            v = scr_ref[h]
            s_out_ref[0, h] = v.T if transpose else v


def _gla_kernel(*refs, prompt, seg_len, real_lo):
    if prompt:
        (uq, uk, uv, ur, us, wg, bg, gn, y_ref, so_ref, scr) = refs
        si_ref = None
        _prompt_state_prologue(scr, [])
    else:
        (uq, uk, uv, ur, us, wg, bg, gn, si_ref, y_ref, so_ref) = refs
        scr = None
    geom = _TileGeom(seg_len, real_lo)
    get, put = _state_access(prompt, geom, si_ref, so_ref, scr, transpose=True)

    logit = _mm3(us[...], wg[...]) + bg[...]
    log_a = geom.keep_valid(-_softplus(-logit) * (1.0 / GLA_GATE_TEMP))
    b = _mm01(geom.causal01, log_a)
    btot = _mm01(geom.all01, log_a)
    q = geom.keep_valid(uq[...]) * (HEAD_W ** -0.5)
    k = geom.keep_valid(uk[...])
    v = geom.keep_valid(uv[...])
    qd = q * jnp.exp(b)
    kd = k * jnp.exp(-b)
    kdec = k * jnp.exp(btot - b)
    edec = jnp.exp(btot)
    gnorm = gn[...]
    for h in range(N_HEADS):
        sl = slice(h * HEAD_W, (h + 1) * HEAD_W)
        a = jnp.where(geom.causal, _mm_nt(qd[:, sl], kd[:, sl]), 0.0)
        o = _mm(a, v[:, sl])
        vt = v[:, sl].T
        for s in range(geom.nseg):
            st = get(s, h)
            o = o + _mm_nt(geom.rows(s, qd[:, sl]), st)
            r0 = s * geom.seg_len
            put(s, h, st * edec[r0:r0 + 1, sl] + _mm(geom.cols(s, vt), kdec[:, sl]))
        y_ref[:, sl] = (_rms(o, gnorm) * _silu(ur[:, sl])).astype(y_ref.dtype)
    if prompt:
        _prompt_state_epilogue(scr, so_ref, transpose=True)


def _ssd_kernel(*refs, prompt, seg_len, real_lo):
    if prompt:
        (ux, ubc, uz, us, cw, cb, dtb, alog, lmask, drow, gn, y_ref, so_ref, co_x, co_bc,
         scr, hist_x, hist_bc) = refs
        si_ref = cx_ref = cbc_ref = None
        _prompt_state_prologue(scr, [hist_x, hist_bc])
    else:
        (ux, ubc, uz, us, cw, cb, dtb, alog, lmask, drow, gn, si_ref, cx_ref, cbc_ref,
         y_ref, so_ref, co_x, co_bc) = refs
        scr = hist_x = hist_bc = None
    geom = _TileGeom(seg_len, real_lo)
    get, put = _state_access(prompt, geom, si_ref, so_ref, scr, transpose=False)
    w = BRANCH_W

    dt = geom.keep_valid(_softplus(us[...] + dtb[...])) * lmask[...]
    dta = dt * (-jnp.exp(alog[...]))
    cum = _mm01(geom.causal01, dta)
    tot = _mm01(geom.all01, dta)
    ecum = jnp.exp(cum)
    wts = jnp.exp(tot - cum) * dt
    etot = jnp.exp(tot)
    cum_t = cum.T
    dt_t = dt.T

    def conv_part(u_ref, cache_ref, hist_ref, co_ref, wsl):
        if prompt:
            xin = u_ref[...]
            hist = hist_ref[...]
        else:
            xin = _conv_input(geom, u_ref[...], cache_ref[...].reshape(TILE, w))
            hist = jnp.zeros((SAMPLE_SEG, w), F32)
        out = _conv_tile(xin, hist, cw[:, wsl], cb[:, wsl])
        tail = xin[TILE - geom.nseg * SAMPLE_SEG:]
        co_ref[...] = tail.reshape(co_ref.shape)
        if prompt:
            hist_ref[...] = tail
        return geom.keep_valid(_silu(out))

    x = conv_part(ux, cx_ref, hist_x, co_x, slice(0, w))
    bc = conv_part(ubc, cbc_ref, hist_bc, co_bc, slice(w, 2 * w))
    z = uz[...]
    lane = lax.broadcasted_iota(jnp.int32, (1, HEAD_W), 1)
    lo_half = lane < SSD_HEADDIM
    row2 = lax.broadcasted_iota(jnp.int32, (2 * SSD_HEADDIM, 1), 0)
    top_half = row2 < SSD_HEADDIM

    for g in range(SSD_GROUPS):
        bg_ = bc[:, g * HEAD_W:(g + 1) * HEAD_W]
        cg_ = bc[:, (SSD_GROUPS + g) * HEAD_W:(SSD_GROUPS + g + 1) * HEAD_W]
        cb_ts = _mm_nt(cg_, bg_)
        ys_pairs = []
        for pr in range(N_HEADS // SSD_GROUPS):
            p = g * (N_HEADS // SSD_GROUPS) + pr
            sl = slice(p * HEAD_W, (p + 1) * HEAD_W)
            xp = x[:, sl]
            lanes = (SM_DT + 2 * p, SM_DT + 2 * p + 1)
            y = None
            for ln, half in zip(lanes, (lo_half, jnp.logical_not(lo_half))):
                decay = jnp.where(geom.causal, jnp.exp(cum[:, ln:ln + 1] - cum_t[ln:ln + 1, :]), 0.0)
                m = cb_ts * decay * dt_t[ln:ln + 1, :]
                term = _mm(m, jnp.where(half, xp, 0.0))
                y = term if y is None else y + term
            ecum_pair = jnp.where(lo_half, ecum[:, lanes[0]:lanes[0] + 1], ecum[:, lanes[1]:lanes[1] + 1])
            wts_pair = jnp.where(lo_half, wts[:, lanes[0]:lanes[0] + 1], wts[:, lanes[1]:lanes[1] + 1])
            xw_t = (xp * wts_pair).T
            inter = None
            for s in range(geom.nseg):
                st = get(s, p)
                term = _mm_nt(geom.rows(s, cg_), st)
                inter = term if inter is None else inter + term
                fac = jnp.where(top_half, geom.seg_scalar(s, etot, lanes[0]), geom.seg_scalar(s, etot, lanes[1]))
                put(s, p, st * fac + _mm(geom.cols(s, xw_t), bg_))
            y = y + inter * ecum_pair
            y = (y + xp * drow[:, sl]) * _silu(z[:, sl])
            ys_pairs.append(y)
        ms = sum(jnp.sum(y * y, axis=-1, keepdims=True) for y in ys_pairs) * (1.0 / (len(ys_pairs) * HEAD_W))
        inv = lax.rsqrt(ms + EPS)
        for pr, y in enumerate(ys_pairs):
            p = g * (N_HEADS // SSD_GROUPS) + pr
            sl = slice(p * HEAD_W, (p + 1) * HEAD_W)
            y_ref[:, sl] = (y * inv * gn[:, sl]).astype(y_ref.dtype)
    if prompt:
        _prompt_state_epilogue(scr, so_ref, transpose=False)


def _gdn_kernel(*refs, prompt, seg_len, real_lo):
    if prompt:
        (uq, uk, uv, ug, us, cw, dtb, alog, lmask, gn, y_ref, so_ref, co_q, co_k, co_v,
         scr, hist_q, hist_k, hist_v) = refs
        si_ref = cq_ref = ck_ref = cv_ref = None
        _prompt_state_prologue(scr, [hist_q, hist_k, hist_v])
    else:
        (uq, uk, uv, ug, us, cw, dtb, alog, lmask, gn, si_ref, cq_ref, ck_ref, cv_ref,
         y_ref, so_ref, co_q, co_k, co_v) = refs
        scr = hist_q = hist_k = hist_v = None
    geom = _TileGeom(seg_len, real_lo)
    get, put = _state_access(prompt, geom, si_ref, so_ref, scr, transpose=False)
    w = BRANCH_W

    small = us[...]
    gdec = geom.keep_valid(-jnp.exp(alog[...]) * _softplus(small + dtb[...])) * lmask[...]
    beta = geom.keep_valid(_sigmoid(small))
    cum = _mm01(geom.causal01, gdec)
    tot = _mm01(geom.all01, gdec)
    ecum = jnp.exp(cum)
    etail = jnp.exp(tot - cum)
    etot = jnp.exp(tot)
    cum_t = cum.T
    beta_t = beta.T

    def conv_part(u_ref, cache_ref, hist_ref, co_ref, wsl):
        if prompt:
            xin = u_ref[...]
            hist = hist_ref[...]
        else:
            xin = _conv_input(geom, u_ref[...], cache_ref[...].reshape(TILE, w))
            hist = jnp.zeros((SAMPLE_SEG, w), F32)
        out = _conv_tile(xin, hist, cw[:, wsl], None)
        tail = xin[TILE - geom.nseg * SAMPLE_SEG:]
        co_ref[...] = tail.reshape(co_ref.shape)
        if prompt:
            hist_ref[...] = tail
        return geom.keep_valid(_silu(out))

    q_all = conv_part(uq, cq_ref, hist_q, co_q, slice(0, w))
    k_all = conv_part(uk, ck_ref, hist_k, co_k, slice(w, 2 * w))
    v_all = conv_part(uv, cv_ref, hist_v, co_v, slice(2 * w, 3 * w))
    n_factors = max(1, (geom.n_real - 1).bit_length())

    for h in range(N_HEADS):
        sl = slice(h * HEAD_W, (h + 1) * HEAD_W)
        la, lb = SM_DA + h, SM_DB + h
        qh, kh, vh = q_all[:, sl], k_all[:, sl], v_all[:, sl]
        qn = qh * lax.rsqrt(jnp.sum(qh * qh, axis=-1, keepdims=True) + EPS) * (HEAD_W ** -0.5)
        kn = kh * lax.rsqrt(jnp.sum(kh * kh, axis=-1, keepdims=True) + EPS)
        bcol, brow = beta[:, lb:lb + 1], beta_t[lb:lb + 1, :]
        ccol, crow = cum[:, la:la + 1], cum_t[la:la + 1, :]
        decay = jnp.where(geom.causal, jnp.exp(ccol - crow), 0.0)
        akk = jnp.where(geom.strict, _mm_nt(kn, kn) * decay, 0.0) * bcol * brow
        pw = -akk
        inv = geom.eye + pw
        for _ in range(n_factors - 1):
            pw = _mm3(pw, pw)
            inv = inv + _mm3(inv, pw)
        rhs = jnp.concatenate([vh * bcol, kn * (bcol * ecum[:, la:la + 1])], axis=1)
        sol = _mm3(inv, rhs)
        u_, w_ = sol[:, :HEAD_W], sol[:, HEAD_W:]
        aqk = jnp.where(geom.causal, _mm_nt(qn, kn) * decay, 0.0)
        qe = qn * ecum[:, la:la + 1]
        kdec_t = (kn * etail[:, la:la + 1]).T
        ws = qs = None
        states = []
        for s in range(geom.nseg):
            st = get(s, h)
            states.append(st)
            t1 = _mm(geom.rows(s, w_), st)
            t2 = _mm(geom.rows(s, qe), st)
            ws = t1 if ws is None else ws + t1
            qs = t2 if qs is None else qs + t2
        vnew = u_ - ws
        o = qs + _mm(aqk, vnew)
        for s in range(geom.nseg):
            put(s, h, states[s] * geom.seg_scalar(s, etot, la) + _mm(geom.cols(s, kdec_t), vnew))
        y_ref[:, sl] = (_rms(o, gn[...]) * _silu(ug[:, sl])).astype(y_ref.dtype)
    if prompt:
        _prompt_state_epilogue(scr, so_ref, transpose=False)


def _mixer_grid(prompt, nb, r):
    if prompt:
        assert r % TILE == 0
        n_inner = r // TILE
        return (nb, n_inner), (lambda i, j: i * n_inner + j)
    assert r == SAMPLE_SEG and nb % (TILE // SAMPLE_SEG) == 0
    return (nb // (TILE // SAMPLE_SEG), 1), (lambda i, j: i)


def _u_spec(row_map, width, col_block):
    return pl.BlockSpec((TILE, width), lambda i, j: (row_map(i, j), col_block))


def _const_spec(shape):
    nd = len(shape)
    return pl.BlockSpec(shape, lambda i, j: (0,) * nd)


def _per_seq_spec(prompt, tail_shape):
    n = 1 if prompt else TILE // SAMPLE_SEG
    nd = len(tail_shape)
    return pl.BlockSpec((n,) + tuple(tail_shape), lambda i, j: (i,) + (0,) * nd)


_MIXER_PARAMS = pltpu.CompilerParams(
    dimension_semantics=("arbitrary", "arbitrary"), vmem_limit_bytes=VMEM_LIMIT)


def _gla_call(prompt, nb, r, u, cols, wg_pad, bg, gn, state_in):
    grid, row_map = _mixer_grid(prompt, nb, r)
    w = BRANCH_W
    in_specs = [
        _u_spec(row_map, w, cols["gq"]), _u_spec(row_map, w, cols["gk"]),
        _u_spec(row_map, w, cols["gv"]), _u_spec(row_map, w, cols["gr"]),
        _u_spec(row_map, LANES, cols["small"]),
        _const_spec(wg_pad.shape), _const_spec(bg.shape), _const_spec(gn.shape),
    ]
    args = [u, u, u, u, u, wg_pad, bg, gn]
    state_tail = (N_HEADS, HEAD_W, HEAD_W)
    scratch = []
    if prompt:
        scratch = [pltpu.VMEM(state_tail, F32)]
    else:
        in_specs.append(_per_seq_spec(prompt, state_tail))
        args.append(state_in)
    kern = functools.partial(_gla_kernel, prompt=prompt, seg_len=TILE if prompt else SAMPLE_SEG,
                             real_lo=0 if prompt else SAMPLE_SEG - 4)
    return pl.pallas_call(
        kern,
        out_shape=(jax.ShapeDtypeStruct((nb * r, w), BF16),
                   jax.ShapeDtypeStruct((nb,) + state_tail, F32)),
        grid=grid,
        in_specs=in_specs,
        out_specs=(_u_spec(row_map, w, 0), _per_seq_spec(prompt, state_tail)),
        scratch_shapes=scratch,
        compiler_params=_MIXER_PARAMS,
        name="gla_prompt" if prompt else "gla_sample",
    )(*args)


def _ssd_call(prompt, nb, r, u, cols, cw, cb, dtb, alog, lmask, drow, gn, state_in, cache_pad):
    grid, row_map = _mixer_grid(prompt, nb, r)
    w = BRANCH_W
    in_specs = [
        _u_spec(row_map, w, cols["sx"]), _u_spec(row_map, w, cols["sbc"]),
        _u_spec(row_map, w, cols["sz"]), _u_spec(row_map, LANES, cols["small"]),
        _const_spec(cw.shape), _const_spec(cb.shape), _const_spec(dtb.shape), _const_spec(alog.shape),
        _const_spec(lmask.shape), _const_spec(drow.shape), _const_spec(gn.shape),
    ]
    args = [u, u, u, u, cw, cb, dtb, alog, lmask, drow, gn]
    state_tail = (N_HEADS, HEAD_W, HEAD_W)
    cache_spec = lambda cb_: pl.BlockSpec((1 if prompt else TILE // SAMPLE_SEG, SAMPLE_SEG, w),
                                          lambda i, j, cb_=cb_: (i, 0, cb_))
    scratch = []
    if prompt:
        scratch = [pltpu.VMEM(state_tail, F32), pltpu.VMEM((SAMPLE_SEG, w), F32), pltpu.VMEM((SAMPLE_SEG, w), F32)]
    else:
        in_specs += [_per_seq_spec(prompt, state_tail), cache_spec(0), cache_spec(1)]
        args += [state_in, cache_pad, cache_pad]
    kern = functools.partial(_ssd_kernel, prompt=prompt, seg_len=TILE if prompt else SAMPLE_SEG,
                             real_lo=0 if prompt else SAMPLE_SEG - 4)
    cache_out = jax.ShapeDtypeStruct((nb, SAMPLE_SEG, w), F32)
    return pl.pallas_call(
        kern,
        out_shape=(jax.ShapeDtypeStruct((nb * r, w), BF16),
                   jax.ShapeDtypeStruct((nb,) + state_tail, F32), cache_out, cache_out),
        grid=grid,
        in_specs=in_specs,
        out_specs=(_u_spec(row_map, w, 0), _per_seq_spec(prompt, state_tail),
                   _per_seq_spec(prompt, (SAMPLE_SEG, w)), _per_seq_spec(prompt, (SAMPLE_SEG, w))),
        scratch_shapes=scratch,
        compiler_params=_MIXER_PARAMS,
        name="ssd_prompt" if prompt else "ssd_sample",
    )(*args)


def _gdn_call(prompt, nb, r, u, cols, cw, dtb, alog, lmask, gn, state_in, cache_pad):
    grid, row_map = _mixer_grid(prompt, nb, r)
    w = BRANCH_W
    in_specs = [
        _u_spec(row_map, w, cols["dq"]), _u_spec(row_map, w, cols["dk"]),
        _u_spec(row_map, w, cols["dv"]), _u_spec(row_map, w, cols["dg"]),
        _u_spec(row_map, LANES, cols["small"]),
        _const_spec(cw.shape), _const_spec(dtb.shape), _const_spec(alog.shape),
        _const_spec(lmask.shape), _const_spec(gn.shape),
    ]
    args = [u, u, u, u, u, cw, dtb, alog, lmask, gn]
    state_tail = (N_HEADS, HEAD_W, HEAD_W)
    cache_spec = lambda cb_: pl.BlockSpec((1 if prompt else TILE // SAMPLE_SEG, SAMPLE_SEG, w),
                                          lambda i, j, cb_=cb_: (i, 0, cb_))
    scratch = []
    if prompt:
        scratch = [pltpu.VMEM(state_tail, F32)] + [pltpu.VMEM((SAMPLE_SEG, w), F32)] * 3
    else:
        in_specs += [_per_seq_spec(prompt, state_tail), cache_spec(0), cache_spec(1), cache_spec(2)]
        args += [state_in, cache_pad, cache_pad, cache_pad]
    kern = functools.partial(_gdn_kernel, prompt=prompt, seg_len=TILE if prompt else SAMPLE_SEG,
                             real_lo=0 if prompt else SAMPLE_SEG - 4)
    cache_out = jax.ShapeDtypeStruct((nb, SAMPLE_SEG, w), F32)
    return pl.pallas_call(
        kern,
        out_shape=(jax.ShapeDtypeStruct((nb * r, w), BF16),
                   jax.ShapeDtypeStruct((nb,) + state_tail, F32), cache_out, cache_out, cache_out),
        grid=grid,
        in_specs=in_specs,
        out_specs=(_u_spec(row_map, w, 0), _per_seq_spec(prompt, state_tail))
        + (_per_seq_spec(prompt, (SAMPLE_SEG, w)),) * 3,
        scratch_shapes=scratch,
        compiler_params=_MIXER_PARAMS,
        name="gdn_prompt" if prompt else "gdn_sample",
    )(*args)


FFN_CHUNKS = 2


def _out_kernel(x_ref, y0, y1, y2, m0, m1, m2, gt_m, sh_f, sc_f, gt_f, g_pm, g_pf, g_of,
                wb, wo, wfi, wfo, o_ref):
    nb, r, d = x_ref.shape
    rows = nb * r

    def mod(v, scale_ref, fn):
        return fn(v.reshape(nb, r, d), scale_ref[...]).reshape(rows, d)

    merged = None
    for n, (y, m) in enumerate(((y0, m0), (y1, m1), (y2, m2))):
        term = _sigmoid(m[...]) * _dot(y[...], wb[n])
        merged = term if merged is None else merged + term
    mix = _dot(merged.astype(BF16), wo[...])
    x1 = x_ref[...].reshape(rows, d) + mod(_rms(mix, g_pm[...]), gt_m, lambda v, g: v * g)
    h = mod(_rms(x1, g_pf[...]), sc_f, lambda v, s: v * (1.0 + s))
    h = mod(h, sh_f, lambda v, s: v + s).astype(BF16)
    hidden = wfo.shape[0]
    ck = hidden // FFN_CHUNKS
    f = None
    for c in range(FFN_CHUNKS):
        a = _dot(h, wfi[:, c * ck:(c + 1) * ck])
        b = _dot(h, wfi[:, hidden + c * ck:hidden + (c + 1) * ck])
        term = _dot((_silu(a) * b).astype(BF16), wfo[c * ck:(c + 1) * ck, :])
        f = term if f is None else f + term
    x2 = x1 + mod(_rms(f, g_of[...]), gt_f, lambda v, g: v * g)
    o_ref[...] = x2.reshape(nb, r, d)


def _out_call(x3, ys, u, cols, mod3, gains, wb, wo, wfi, wfo):
    nb, r, d = x3.shape
    nbk, rk = _row_blocking(x3, 256)
    gi, gj = nb // nbk, r // rk
    rows = nbk * rk
    row_map = lambda i, j: i * gj + j
    w = BRANCH_W
    y_spec = pl.BlockSpec((rows, w), lambda i, j: (row_map(i, j), 0))
    mg_spec = lambda n: pl.BlockSpec((rows, d), lambda i, j, n=n: (row_map(i, j), cols["mg"] + n))
    mod_spec = lambda col: pl.BlockSpec((nbk, 1, d), lambda i, j, col=col: (i, 0, col))
    gain_spec = pl.BlockSpec((1, d), lambda i, j: (0, 0))

    def resident(shape):
        nd = len(shape)
        return pl.BlockSpec(shape, lambda i, j: (0,) * nd, pipeline_mode=pl.Buffered(1))

    g_pm, g_pf, g_of = gains
    return pl.pallas_call(
        _out_kernel,
        out_shape=jax.ShapeDtypeStruct((nb, r, d), F32),
        grid=(gi, gj),
        in_specs=[
            pl.BlockSpec((nbk, rk, d), lambda i, j: (i, j, 0)),
            y_spec, y_spec, y_spec, mg_spec(0), mg_spec(1), mg_spec(2),
            mod_spec(2), mod_spec(3), mod_spec(4), mod_spec(5),
            gain_spec, gain_spec, gain_spec,
            resident(wb.shape), resident(wo.shape), resident(wfi.shape), resident(wfo.shape),
        ],
        out_specs=pl.BlockSpec((nbk, rk, d), lambda i, j: (i, j, 0)),
        compiler_params=pltpu.CompilerParams(
            dimension_semantics=("arbitrary", "arbitrary"), vmem_limit_bytes=VMEM_LIMIT),
        name="out_stage",
    )(x3, ys[0], ys[1], ys[2], u, u, u, mod3, mod3, mod3, mod3,
      g_pm.reshape(1, d), g_pf.reshape(1, d), g_of.reshape(1, d), wb, wo, wfi, wfo)


def _in_proj_layout(d):
    w = BRANCH_W
    o = {}
    pos = 0
    for name, width in (("gq", w), ("gk", w), ("gv", w), ("gr", w), ("glr", GLA_RANK), ("sz", w),
                        ("sxbc", 2 * w), ("sdt", SSD_HEADS), ("dqkv", 3 * w), ("da", N_HEADS),
                        ("db", N_HEADS), ("dg", w), ("mg", N_BRANCH * d)):
        o[name] = (pos, pos + width)
        pos += width
    order = ["gq", "gk", "gv", "gr", "dqkv", "dg", "mg", "sxbc", "sz"]
    small = ["glr", "sdt", "da", "db"]
    cols = {"gq": 0, "gk": 1, "gv": 2, "gr": 3, "dq": 4, "dk": 5, "dv": 6, "dg": 7,
            "mg": 8 * w // d, "sx": (8 * w + N_BRANCH * d) // w, "sbc": (8 * w + N_BRANCH * d) // w + 1,
            "sz": (10 * w + N_BRANCH * d) // w, "small": (11 * w + N_BRANCH * d) // LANES}
    return o, order, small, cols


def _relayout_w_in(w_in):
    depth, d, _ = w_in.shape
    o, order, small, cols = _in_proj_layout(d)
    parts = [w_in[:, :, o[n][0]:o[n][1]] for n in order]
    sm = [w_in[:, :, o[n][0]:o[n][1]] for n in small]
    n_small = sum(p.shape[-1] for p in sm)
    sm.append(jnp.zeros((depth, d, LANES - n_small), w_in.dtype))
    return jnp.concatenate(parts + sm, axis=-1).astype(BF16), cols


def _lane_row(vals, start):
    return jnp.zeros((1, LANES), F32).at[0, start:start + vals.shape[0]].set(vals.astype(F32))


def _trunk(x3, mod3, prompt, states, p, cols):
    nb, r, d = x3.shape
    depth = p["w_in_r"].shape[0]
    outs = [[] for _ in range(5)]
    for l in range(depth):
        ml = mod3[l]
        u = _inproj_call(x3, ml, p["g_pre_mix"][l], p["w_in_r"][l])
        if prompt:
            st = (None,) * 5
        else:
            st = tuple(s[l] for s in states)
        y_gla, s_gla = _gla_call(prompt, nb, r, u, cols, p["wg_pad"][l], p["b_gla_gate"][l][None],
                                 p["g_gla_norm"][l][None], st[0])
        y_ssd, s_ssd, c_sx, c_sbc = _ssd_call(
            prompt, nb, r, u, cols, p["w_ssd_conv"][l], p["b_ssd_conv"][l][None], p["ssd_dtb_row"][l],
            p["ssd_alog_row"][l], p["ssd_mask_row"], p["ssd_d_row"][l], p["g_ssd_norm"][l][None], st[1], st[2])
        y_gdn, s_gdn, c_q, c_k, c_v = _gdn_call(
            prompt, nb, r, u, cols, p["w_gdn_conv"][l], p["gdn_dtb_row"][l], p["gdn_alog_row"][l],
            p["gdn_mask_row"], p["g_gdn_norm"][l][None], st[3], st[4])
        x3 = _out_call(x3, (y_gla, y_ssd, y_gdn), u, cols, ml,
                       (p["g_post_mix"][l], p["g_pre_ffn"][l], p["g_post_ffn"][l]),
                       p["w_branch"][l], p["w_out"][l], p["w_ffn_in"][l], p["w_ffn_out"][l])
        keep = slice(SAMPLE_SEG - (CONV_W - 1), SAMPLE_SEG)
        outs[0].append(s_gla)
        outs[1].append(s_ssd.reshape(nb, SSD_HEADS, SSD_HEADDIM, HEAD_W))
        outs[2].append(jnp.concatenate([c_sx, c_sbc], axis=-1)[:, keep])
        outs[3].append(s_gdn)
        outs[4].append(jnp.concatenate([c_q, c_k, c_v], axis=-1)[:, keep])
    return x3, tuple(jnp.stack(o) for o in outs)


def _pad_cache(cache):
    lo = SAMPLE_SEG - 4 - (CONV_W - 1)
    return jnp.pad(cache, ((0, 0), (0, 0), (lo, SAMPLE_SEG - lo - (CONV_W - 1)), (0, 0)))


def kernel(x_prompt, x_sample, state_gla, state_ssd, cache_ssd_conv, state_gdn, cache_gdn_conv, c_prompt, c_sample, w_ada, b_ada, g_pre_mix, g_post_mix, g_pre_ffn, g_post_ffn, w_in, w_gla_gate, b_gla_gate, g_gla_norm, w_ssd_conv, b_ssd_conv, ssd_dt_bias, ssd_a_log, ssd_d, g_ssd_norm, w_gdn_conv, gdn_dt_bias, gdn_a_log, g_gdn_norm, w_branch, w_out, w_ffn_in, w_ffn_out):
    depth, d, _ = w_in.shape
    n_p = x_prompt.shape[0]
    n_s, l_s, _ = x_sample.shape
    assert l_s == 4

    w_in_r, cols = _relayout_w_in(w_in)
    rank = w_gla_gate.shape[1]
    p = dict(
        w_in_r=w_in_r,
        g_pre_mix=g_pre_mix, g_post_mix=g_post_mix, g_pre_ffn=g_pre_ffn, g_post_ffn=g_post_ffn,
        wg_pad=jnp.pad(w_gla_gate, ((0, 0), (0, LANES - rank), (0, 0))),
        b_gla_gate=b_gla_gate, g_gla_norm=g_gla_norm,
        w_ssd_conv=w_ssd_conv, b_ssd_conv=b_ssd_conv,
        ssd_dtb_row=jnp.stack([_lane_row(ssd_dt_bias[l], SM_DT) for l in range(depth)]),
        ssd_alog_row=jnp.stack([_lane_row(ssd_a_log[l], SM_DT) for l in range(depth)]),
        ssd_mask_row=_lane_row(jnp.ones((SSD_HEADS,), F32), SM_DT),
        ssd_d_row=jnp.repeat(ssd_d, SSD_HEADDIM, axis=-1)[:, None, :],
        g_ssd_norm=g_ssd_norm,
        w_gdn_conv=w_gdn_conv,
        gdn_dtb_row=jnp.stack([_lane_row(gdn_dt_bias[l], SM_DA) for l in range(depth)]),
        gdn_alog_row=jnp.stack([_lane_row(gdn_a_log[l], SM_DA) for l in range(depth)]),
        gdn_mask_row=_lane_row(jnp.ones((N_HEADS,), F32), SM_DA),
        g_gdn_norm=g_gdn_norm,
        w_branch=w_branch.astype(BF16), w_out=w_out.astype(BF16),
        w_ffn_in=w_ffn_in.astype(BF16), w_ffn_out=w_ffn_out.astype(BF16),
    )

    mod = _ada_call(jnp.concatenate([c_prompt, c_sample], axis=0), w_ada, b_ada)
    mod_p = mod[:, :n_p, None, :]
    mod_s = mod[:, n_p:, None, :]

    y_p, st_p = _trunk(x_prompt, mod_p, True, None, p, cols)

    x_s = jnp.pad(x_sample, ((0, 0), (SAMPLE_SEG - l_s, 0), (0, 0)))
    states_s = (state_gla, state_ssd.reshape(depth, n_s, N_HEADS, HEAD_W, HEAD_W), _pad_cache(cache_ssd_conv),
                state_gdn, _pad_cache(cache_gdn_conv))
    y_s, st_s = _trunk(x_s, mod_s, False, states_s, p, cols)
    y_s = y_s[:, SAMPLE_SEG - l_s:]

    return (y_p,) + (y_s,) + st_p + st_s
```
